```python
import math
import jax
import jax.numpy as jnp
from jax import lax
import numpy as np

D_MODEL = 1024
BATCH = 4
SEQ = 8192
DEPTH = 2

CTX_LEN = 256
GRID_W = 64
HEAD_DIM = 64
BRANCH_W = D_MODEL // 2

ATT_HEADS = BRANCH_W // HEAD_DIM
ATT_KV_HEADS = 2
ATT_GROUP = ATT_HEADS // ATT_KV_HEADS
WINDOW = 128
BLOCK = 128
ROPE_BASE = 10000.0
NEG_INF = -1e30

RWKV_HEADS = BRANCH_W // HEAD_DIM
RWKV_W = RWKV_HEADS * HEAD_DIM
W_LORA = 64
A_LORA = 64
G_LORA = 128
DECAY_SCALE = 0.606531
GN_EPS = 64e-5

S5_W = BRANCH_W
S5_GROUP_CH = 16
S5_GROUPS = S5_W // S5_GROUP_CH
S5_STATE = 64

D_FF = 2816
FFN_RES = 0.5
N_BRANCH = 3
N_MOD = 9
ALPHA = (2.0 * DEPTH) ** 0.25
BETA = (8.0 * DEPTH) ** -0.25
LN_EPS = 1e-6

ATT_Q = ATT_HEADS * HEAD_DIM
ATT_KV = ATT_KV_HEADS * HEAD_DIM
ATT_IN = ATT_Q + 2 * ATT_KV
RWKV_IN = 3 * RWKV_W + 2 * W_LORA + A_LORA + G_LORA
N_IN = ATT_IN + RWKV_IN + S5_W + N_BRANCH * D_MODEL

F32 = jnp.float32

kernel_name = 'hybrid_dit_gqa_rwkv7_s5_block'


def _normalise(x):
    xf = x.astype(F32)
    mean = jnp.mean(xf, -1, keepdims=True)
    var = jnp.mean(jnp.square(xf - mean), -1, keepdims=True)
    return (xf - mean) * lax.rsqrt(var + LN_EPS)


def ln_plain(x):
    return _normalise(x).astype(x.dtype)


def ln_affine(x, g, b):
    return (_normalise(x) * g.astype(F32) + b.astype(F32)).astype(x.dtype)


def modulate(x, shift, scale):
    return ln_plain(x) * (1.0 + scale) + shift


def swiglu(u, w1, w2):
    gate, up = jnp.split(u @ w1, 2, axis=-1)
    return (jax.nn.silu(gate) * up) @ w2


def ffn_sublayer(x, shift, scale, gate, w1, w2, g, b):
    h = swiglu(modulate(x, shift, scale), w1, w2)
    return ln_affine(ALPHA * x + FFN_RES * gate * h, g, b)


def centred_shift(z):
    zp = jnp.pad(z, ((0, 0), (1, 1), (0, 0)))
    return 0.5 * (zp[:, :-2] + zp[:, 2:])


def rope_1d(x, pos):
    n = x.shape[-1] // 2
    inv = ROPE_BASE ** (-jnp.arange(n, dtype=F32) / n)
    ang = pos.astype(F32)[:, None] * inv[None, :]
    shape = (pos.shape[0],) + (1,) * (x.ndim - 3) + (n,)
    cos = jnp.cos(ang).reshape(shape)
    sin = jnp.sin(ang).reshape(shape)
    xf = x.astype(F32)
    x1, x2 = xf[..., :n], xf[..., n:]
    return jnp.concatenate([x1 * cos - x2 * sin, x2 * cos + x1 * sin], axis=-1).astype(x.dtype)


def axial_rope(x, row, col):
    h = HEAD_DIM // 2
    return jnp.concatenate([rope_1d(x[..., :h], row), rope_1d(x[..., h:], col)], axis=-1)


def attention_branch(zl, zc, sink, row, col, ctx_out):
    bsz, t_len = zl.shape[:2]
    c_len = zc.shape[1]
    nb = t_len // BLOCK
    scale = HEAD_DIM ** -0.5

    def split_heads(z):
        q, k, v = jnp.split(z, [ATT_Q, ATT_Q + ATT_KV], axis=-1)
        lead = z.shape[:2]
        return (q.reshape(lead + (ATT_KV_HEADS, ATT_GROUP, HEAD_DIM)),
                k.reshape(lead + (ATT_KV_HEADS, HEAD_DIM)),
                v.reshape(lead + (ATT_KV_HEADS, HEAD_DIM)))

    ql, kl, vl = split_heads(zl)
    qc, kc, vc = split_heads(zc)
    ql = axial_rope(ql, row, col)
    kl = axial_rope(kl, row, col)

    def band(t):
        tp = jnp.pad(t, ((0, 0), (BLOCK, BLOCK), (0, 0), (0, 0)))
        tp = tp.reshape(bsz, nb + 2, BLOCK, ATT_KV_HEADS, HEAD_DIM)
        return jnp.concatenate([tp[:, :-2], tp[:, 1:-1], tp[:, 2:]], axis=2)

    qb = ql.reshape(bsz, nb, BLOCK, ATT_KV_HEADS, ATT_GROUP, HEAD_DIM)
    kb, vb = band(kl), band(vl)
    s_loc = jnp.einsum('bnqhgd,bnkhd->bhgnqk', qb, kb).astype(F32) * scale
    q_off = jnp.arange(BLOCK)[:, None]
    k_off = jnp.arange(3 * BLOCK)[None, :]
    k_pos = jnp.arange(nb)[:, None, None] * BLOCK + k_off[None] - BLOCK
    mask = (jnp.abs(k_off - BLOCK - q_off) <= WINDOW)[None] & (k_pos >= 0) & (k_pos < t_len)
    s_loc = jnp.where(mask, s_loc, NEG_INF)
    s_ctx = jnp.einsum('bnqhgd,bchd->bhgnqc', qb, kc).astype(F32) * scale
    sink_f = sink.astype(F32).reshape(1, ATT_KV_HEADS, ATT_GROUP, 1, 1, 1)
    s_sink = jnp.broadcast_to(sink_f, s_loc.shape[:-1] + (1,))
    p = jax.nn.softmax(jnp.concatenate([s_loc, s_ctx, s_sink], axis=-1), axis=-1)
    p_loc = p[..., :3 * BLOCK].astype(vb.dtype)
    p_ctx = p[..., 3 * BLOCK:3 * BLOCK + c_len].astype(vc.dtype)
    ol = (jnp.einsum('bhgnqk,bnkhd->bnqhgd', p_loc, vb)
          + jnp.einsum('bhgnqc,bchd->bnqhgd', p_ctx, vc))
    yl = ol.reshape(bsz, t_len, ATT_Q)
    yc = None
    if ctx_out:
        sc = jnp.einsum('bchgd,bkhd->bhgck', qc, kc).astype(F32) * scale
        sc_sink = jnp.broadcast_to(sink_f[..., 0], sc.shape[:-1] + (1,))
        pc = jax.nn.softmax(jnp.concatenate([sc, sc_sink], axis=-1), axis=-1)[..., :c_len]
        yc = jnp.einsum('bhgck,bkhd->bchgd', pc.astype(vc.dtype), vc).reshape(bsz, c_len, ATT_Q)
    return yl, yc


def rwkv7_scan(state0, r, decay, k, v, kk, kka, reverse):
    def step(state, inp):
        r_t, w_t, k_t, v_t, kk_t, b_t = inp
        state = (state * w_t[:, :, None, :]
                 - jnp.einsum('bhvk,bhk->bhv', state, kk_t)[..., None] * b_t[:, :, None, :]
                 + v_t[..., None] * k_t[:, :, None, :])
        return state, jnp.einsum('bhvk,bhk->bhv', state, r_t)

    xs = tuple(jnp.swapaxes(t, 0, 1) for t in (r, decay, k, v, kk, kka))
    state, out = lax.scan(step, state0, xs, reverse=reverse)
    return state, jnp.swapaxes(out, 0, 1)


def rwkv_branch(zl, zc, mu, w0, w2, a0, a2, g2, k_k, k_a, r_k, gn_g, gn_b, ctx_out):
    cuts = [RWKV_W, 2 * RWKV_W, 3 * RWKV_W, 3 * RWKV_W + 2 * W_LORA,
            3 * RWKV_W + 2 * W_LORA + A_LORA]
    mu_f, w0_f, w2_f = mu.astype(F32), w0.astype(F32), w2.astype(F32)
    a0_f, a2_f, g2_f = a0.astype(F32), a2.astype(F32), g2.astype(F32)
    kk_f, ka_f, rk_f = k_k.astype(F32), k_a.astype(F32), r_k.astype(F32)
    gng_f, gnb_f = gn_g.astype(F32), gn_b.astype(F32)

    def prepare(z):
        bsz, t_len = z.shape[:2]
        zf = z.astype(F32)
        zf = zf + mu_f * (centred_shift(zf) - zf)
        r, k, v, w_lo, a_lo, g_lo = jnp.split(zf, cuts, axis=-1)
        heads = lambda t: t.reshape(bsz, t_len, RWKV_HEADS, HEAD_DIM)
        w_lo = w_lo.reshape(bsz, t_len, 2, W_LORA)
        decay = jnp.exp(-DECAY_SCALE * jax.nn.sigmoid(
            w0_f + jnp.einsum('btdr,drc->btdc', jnp.tanh(w_lo), w2_f)))
        a = jax.nn.sigmoid(a0_f + a_lo @ a2_f)
        g = jax.nn.sigmoid(g_lo) @ g2_f
        kk = heads(k * kk_f)
        kk = kk * lax.rsqrt(jnp.sum(jnp.square(kk), -1, keepdims=True) + 1e-12)
        k = k * (1.0 + (a - 1.0) * ka_f)
        return (heads(r), heads(decay[:, :, 0]), heads(decay[:, :, 1]), heads(k), heads(v),
                kk, kk * heads(a), g)

    def finish(o, r, k, v, g, dtype):
        bsz, t_len = o.shape[:2]
        mean = jnp.mean(o, -1, keepdims=True)
        var = jnp.mean(jnp.square(o - mean), -1, keepdims=True)
        on = ((o - mean) * lax.rsqrt(var + GN_EPS)).reshape(bsz, t_len, RWKV_W) * gng_f + gnb_f
        bonus = (jnp.sum(r * k * rk_f, -1, keepdims=True) * v).reshape(bsz, t_len, RWKV_W)
        return ((on + bonus) * g).astype(dtype)

    rc, dfc, dbc, kc, vc, kkc, bc, gc = prepare(zc)
    rl, dfl, dbl, kl, vl, kkl, bl, gl = prepare(zl)
    zero = jnp.zeros((zc.shape[0], RWKV_HEADS, HEAD_DIM, HEAD_DIM), F32)
    s_f, ocf = rwkv7_scan(zero, rc, dfc, kc, vc, kkc, bc, reverse=False)
    s_b, ocb = rwkv7_scan(zero, rc, dbc, kc, vc, kkc, bc, reverse=True)
    _, olf = rwkv7_scan(s_f, rl, dfl, kl, vl, kkl, bl, reverse=False)
    _, olb = rwkv7_scan(s_b, rl, dbl, kl, vl, kkl, bl, reverse=True)
    yl = finish(olf + olb, rl, kl, vl, gl, zl.dtype)
    yc = finish(ocf + ocb, rc, kc, vc, gc, zc.dtype) if ctx_out else None
    return yl, yc


def s5_discretise(a_re, a_im, log_step, b_re, b_im):
    a_re, a_im = a_re.astype(F32), a_im.astype(F32)
    dt = jnp.exp(log_step.astype(F32))[:, None]
    mag = jnp.exp(a_re * dt)
    lr, li = mag * jnp.cos(a_im * dt), mag * jnp.sin(a_im * dt)
    den = jnp.square(a_re) + jnp.square(a_im)
    cr = ((lr - 1.0) * a_re + li * a_im) / den
    ci = (li * a_re - (lr - 1.0) * a_im) / den
    br, bi = b_re.astype(F32), b_im.astype(F32)
    bbr = cr[..., None] * br - ci[..., None] * bi
    bbi = cr[..., None] * bi + ci[..., None] * br
    return lr, li, bbr, bbi


def s5_combine(e1, e2):
    a1r, a1i, b1r, b1i = e1
    a2r, a2i, b2r, b2i = e2
    return (a2r * a1r - a2i * a1i, a2r * a1i + a2i * a1r,
            a2r * b1r - a2i * b1i + b2r, a2r * b1i + a2i * b1r + b2i)


def s5_scan_fwd(lr, li, br, bi, h0r, h0i):
    br = br.at[:, 0].add(lr * h0r - li * h0i)
    bi = bi.at[:, 0].add(lr * h0i + li * h0r)
    shape = (1, br.shape[1]) + lr.shape
    ar = jnp.broadcast_to(lr, shape)
    ai = jnp.broadcast_to(li, shape)
    _, _, xr, xi = lax.associative_scan(s5_combine, (ar, ai, br, bi), axis=1)
    return xr, xi


def s5_scan_bwd(lr, li, br, bi, h0r, h0i):
    xr, xi = s5_scan_fwd(lr, li, jnp.flip(br, 1), jnp.flip(bi, 1), h0r, h0i)
    return jnp.flip(xr, 1), jnp.flip(xi, 1)


def s5_branch(ul, uc, a_re, a_im, log_step, b_re, b_im, c_re, c_im, d, glu_w, glu_b, ctx_out):
    fwd = s5_discretise(a_re[0], a_im[0], log_step[0], b_re, b_im)
    bwd = s5_discretise(a_re[1], a_im[1], log_step[1], b_re, b_im)
    cr, ci = c_re.astype(F32), c_im.astype(F32)

    def drive(u, disc):
        ug = u.astype(F32).reshape(u.shape[:2] + (S5_GROUPS, S5_GROUP_CH))
        return (jnp.einsum('btgc,gpc->btgp', ug, disc[2]),
                jnp.einsum('btgc,gpc->btgp', ug, disc[3]))

    def read(xr, xi):
        return jnp.einsum('btgp,gcp->btgc', xr, cr) - jnp.einsum('btgp,gcp->btgc', xi, ci)

    def finish(u, y):
        y = y.reshape(u.shape[:2] + (S5_W,)) + d.astype(F32) * u.astype(F32)
        y = jax.nn.gelu(y)
        return (y * jax.nn.sigmoid(y @ glu_w.astype(F32) + glu_b.astype(F32))).astype(u.dtype)

    zero = jnp.zeros((uc.shape[0], S5_GROUPS, S5_STATE), F32)
    xcf = s5_scan_fwd(fwd[0], fwd[1], *drive(uc, fwd), zero, zero)
    xcb = s5_scan_bwd(bwd[0], bwd[1], *drive(uc, bwd), zero, zero)
    ylf = read(*s5_scan_fwd(fwd[0], fwd[1], *drive(ul, fwd), xcf[0][:, -1], xcf[1][:, -1]))
    ylb = read(*s5_scan_bwd(bwd[0], bwd[1], *drive(ul, bwd), xcb[0][:, 0], xcb[1][:, 0]))
    yl = finish(ul, ylf + ylb)
    yc = finish(uc, read(*xcf) + read(*xcb)) if ctx_out else None
    return yl, yc


def gated_merge(z_gate, ya, yr, ys, branch_proj, w_out):
    ga, gr, gs = jnp.split(jax.nn.sigmoid(z_gate), N_BRANCH, axis=-1)
    m = ga * (ya @ branch_proj[0]) + gr * (yr @ branch_proj[1]) + gs * (ys @ branch_proj[2])
    return m @ w_out


def mixer_sublayer(ul, uc, row, col, ctx_out, w_in, attn_sink, rwkv_mu, rwkv_w0, rwkv_w2,
                   rwkv_a0, rwkv_a2, rwkv_g2, rwkv_k_k, rwkv_k_a, rwkv_r_k, rwkv_gn_g, rwkv_gn_b,
                   s5_a_re, s5_a_im, s5_log_step, s5_b_re, s5_b_im, s5_c_re, s5_c_im, s5_d,
                   s5_glu_w, s5_glu_b, branch_proj, w_out):
    cuts = [ATT_IN, ATT_IN + RWKV_IN, ATT_IN + RWKV_IN + S5_W]
    za_l, zr_l, zs_l, zg_l = jnp.split(ul @ w_in, cuts, axis=-1)
    za_c, zr_c, zs_c, zg_c = jnp.split(uc @ w_in, cuts, axis=-1)
    ya_l, ya_c = attention_branch(za_l, za_c, attn_sink, row, col, ctx_out)
    yr_l, yr_c = rwkv_branch(zr_l, zr_c, rwkv_mu, rwkv_w0, rwkv_w2, rwkv_a0, rwkv_a2, rwkv_g2,
                             rwkv_k_k, rwkv_k_a, rwkv_r_k, rwkv_gn_g, rwkv_gn_b, ctx_out)
    ys_l, ys_c = s5_branch(zs_l, zs_c, s5_a_re, s5_a_im, s5_log_step, s5_b_re, s5_b_im,
                           s5_c_re, s5_c_im, s5_d, s5_glu_w, s5_glu_b, ctx_out)
    out_l = gated_merge(zg_l, ya_l, yr_l, ys_l, branch_proj, w_out)
    out_c = gated_merge(zg_c, ya_c, yr_c, ys_c, branch_proj, w_out) if ctx_out else None
    return out_l, out_c


def setup_inputs(seed: int = 0) -> dict:
    key = jax.random.key(seed)
    keys = jax.random.split(key, 40)

    def nrm(i, shape, scale):
        return scale * jax.random.normal(keys[i], shape, F32)

    def uni(i, shape, lo, hi):
        return jax.random.uniform(keys[i], shape, F32, lo, hi)

    D = D_MODEL
    L = DEPTH
    n_idx = jnp.arange(S5_STATE, dtype=F32)
    s5_shape = (L, 2, S5_GROUPS, S5_STATE)
    return {
        'x': nrm(0, (BATCH, SEQ, D), 1.0),
        'c': nrm(1, (BATCH, D), 1.0),
        'ctx': nrm(2, (BATCH, CTX_LEN, D), 1.0),
        'c_ctx': nrm(3, (D,), 1.0),
        'w_ada': nrm(4, (L, D, N_MOD * D), 0.5 * D ** -0.5),
        'b_ada': nrm(5, (L, N_MOD * D), 0.02),
        'ln_g': 1.0 + nrm(6, (L, 3, D), 0.05),
        'ln_b': nrm(7, (L, 3, D), 0.02),
        'ffn_w_in': nrm(8, (L, 2, D, 2 * D_FF), D ** -0.5),
        'ffn_w_out': nrm(9, (L, 2, D_FF, D), BETA * D_FF ** -0.5),
        'w_in': nrm(10, (L, D, N_IN), D ** -0.5),
        'attn_sink': nrm(11, (L, ATT_HEADS), 0.5),
        'rwkv_mu': uni(12, (L, RWKV_IN), 0.0, 1.0),
        'rwkv_w0': uni(13, (L, 2, RWKV_W), -2.0, 2.0),
        'rwkv_w2': nrm(14, (L, 2, W_LORA, RWKV_W), 0.5 * W_LORA ** -0.5),
        'rwkv_a0': nrm(15, (L, RWKV_W), 0.1),
        'rwkv_a2': nrm(16, (L, A_LORA, RWKV_W), 0.5 * A_LORA ** -0.5),
        'rwkv_g2': nrm(17, (L, G_LORA, RWKV_W), G_LORA ** -0.5),
        'rwkv_k_k': 0.85 + nrm(18, (L, RWKV_W), 0.05),
        'rwkv_k_a': 1.0 + nrm(19, (L, RWKV_W), 0.05),
        'rwkv_r_k': nrm(20, (L, RWKV_HEADS, HEAD_DIM), 0.1),
        'rwkv_gn_g': 1.0 + nrm(21, (L, RWKV_W), 0.05),
        'rwkv_gn_b': nrm(22, (L, RWKV_W), 0.02),
        's5_a_re': -0.5 * jnp.exp(nrm(23, s5_shape, 0.05)),
        's5_a_im': math.pi * n_idx + nrm(24, s5_shape, 0.01),
        's5_log_step': uni(25, (L, 2, S5_GROUPS), math.log(1e-3), math.log(1e-1)),
        's5_b_re': nrm(26, (L, S5_GROUPS, S5_STATE, S5_GROUP_CH), (2.0 * S5_GROUP_CH) ** -0.5),
        's5_b_im': nrm(27, (L, S5_GROUPS, S5_STATE, S5_GROUP_CH), (2.0 * S5_GROUP_CH) ** -0.5),
        's5_c_re': nrm(28, (L, S5_GROUPS, S5_GROUP_CH, S5_STATE), S5_STATE ** -0.5),
        's5_c_im': nrm(29, (L, S5_GROUPS, S5_GROUP_CH, S5_STATE), S5_STATE ** -0.5),
        's5_d': nrm(30, (L, S5_W), 1.0),
        's5_glu_w': nrm(31, (L, S5_W, S5_W), S5_W ** -0.5),
        's5_glu_b': nrm(32, (L, S5_W), 0.02),
        'branch_proj': nrm(33, (L, N_BRANCH, BRANCH_W, D), BRANCH_W ** -0.5),
        'w_out': nrm(34, (L, D, D), BETA * D ** -0.5),
    }


def reference(x, c, ctx, c_ctx, w_ada, b_ada, ln_g, ln_b, ffn_w_in, ffn_w_out, w_in, attn_sink,
              rwkv_mu, rwkv_w0, rwkv_w2, rwkv_a0, rwkv_a2, rwkv_g2, rwkv_k_k, rwkv_k_a, rwkv_r_k,
              rwkv_gn_g, rwkv_gn_b, s5_a_re, s5_a_im, s5_log_step, s5_b_re, s5_b_im, s5_c_re,
              s5_c_im, s5_d, s5_glu_w, s5_glu_b, branch_proj, w_out):
    t_len = x.shape[1]
    rows = t_len // GRID_W
    row = jnp.repeat(jnp.arange(rows, dtype=jnp.int32), GRID_W)
    col = jnp.tile(jnp.arange(GRID_W, dtype=jnp.int32), rows)
    cond_l = jax.nn.silu(c)
    cond_c = jax.nn.silu(c_ctx)
    xl, xc = x, ctx
    for l in range(DEPTH):
        ctx_out = l < DEPTH - 1
        ml = jnp.split((cond_l @ w_ada[l] + b_ada[l])[:, None, :], N_MOD, axis=-1)
        mc = jnp.split(cond_c @ w_ada[l] + b_ada[l], N_MOD, axis=-1)
        xl = ffn_sublayer(xl, ml[0], ml[1], ml[2], ffn_w_in[l, 0], ffn_w_out[l, 0], ln_g[l, 0], ln_b[l, 0])
        xc = ffn_sublayer(xc, mc[0], mc[1], mc[2], ffn_w_in[l, 0], ffn_w_out[l, 0], ln_g[l, 0], ln_b[l, 0])
        yl, yc = mixer_sublayer(
            modulate(xl, ml[3], ml[4]), modulate(xc, mc[3], mc[4]), row, col, ctx_out,
            w_in[l], attn_sink[l], rwkv_mu[l], rwkv_w0[l], rwkv_w2[l], rwkv_a0[l], rwkv_a2[l],
            rwkv_g2[l], rwkv_k_k[l], rwkv_k_a[l], rwkv_r_k[l], rwkv_gn_g[l], rwkv_gn_b[l],
            s5_a_re[l], s5_a_im[l], s5_log_step[l], s5_b_re[l], s5_b_im[l], s5_c_re[l],
            s5_c_im[l], s5_d[l], s5_glu_w[l], s5_glu_b[l], branch_proj[l], w_out[l])
        xl = ln_affine(ALPHA * xl + ml[5] * yl, ln_g[l, 1], ln_b[l, 1])
        xl = ffn_sublayer(xl, ml[6], ml[7], ml[8], ffn_w_in[l, 1], ffn_w_out[l, 1], ln_g[l, 2], ln_b[l, 2])
        if ctx_out:
            xc = ln_affine(ALPHA * xc + mc[5] * yc, ln_g[l, 1], ln_b[l, 1])
            xc = ffn_sublayer(xc, mc[6], mc[7], mc[8], ffn_w_in[l, 1], ffn_w_out[l, 1], ln_g[l, 2], ln_b[l, 2])
    return xl
```

```python
import functools
import math

import jax
import jax.numpy as jnp
import numpy as np
from jax import lax
from jax.experimental import pallas as pl
from jax.experimental.pallas import tpu as pltpu

F32 = jnp.float32
BF16 = jnp.bfloat16

HEAD_DIM = 64
ATT_KV_HEADS = 2
ATT_GROUP = 4
WINDOW = 128
ATT_BLOCK = 128
ROPE_BASE = 10000.0
GRID_W = 64
NEG_INF = -1e30
W_LORA = 64
A_LORA = 64
G_LORA = 128
DECAY_SCALE = 0.606531
GN_EPS = 64e-5
S5_GROUP_CH = 16
S5_STATE = 64
FFN_RES = 0.5
N_MOD = 9
LN_EPS = 1e-6

LANES = 128
VMEM_LIMIT = 56 * 1024 * 1024

TM_FFN = 768
FFN_CHUNK = 256
TM_PROJ = 256
TM_MERGE = 768
RW_TILE = 256
RW_CHUNK = 64
S5_TILE = 256
S5_CHUNK = 128
S5_MID = 64


def _cparams(sem):
    return pltpu.CompilerParams(dimension_semantics=sem, vmem_limit_bytes=VMEM_LIMIT)


def _const_spec(shape):
    nd = len(shape)
    return pl.BlockSpec(shape, lambda *_: (0,) * nd, pipeline_mode=pl.Buffered(1))


def _dot(a, b):
    return jnp.dot(a, b, preferred_element_type=F32)


def _dot_nt(a, b):
    return lax.dot_general(a, b, (((1,), (1,)), ((), ())), preferred_element_type=F32)


def _dot_tn(a, b):
    return lax.dot_general(a, b, (((0,), (0,)), ((), ())), preferred_element_type=F32)


def _normalise(x):
    mean = jnp.mean(x, axis=-1, keepdims=True)
    xc = x - mean
    var = jnp.mean(xc * xc, axis=-1, keepdims=True)
    return xc * lax.rsqrt(var + LN_EPS)


def _sigmoid(x):
    return 1.0 / (1.0 + jnp.exp(-x))


def _silu(x):
    return x * _sigmoid(x)


def _split_dot(x, w_bf16):
    hi = x.astype(BF16)
    lo = (x - hi.astype(F32)).astype(BF16)
    return _dot(hi, w_bf16) + _dot(lo, w_bf16)


def _row_iota(shape):
    return lax.broadcasted_iota(jnp.int32, shape, 0)


def _col_iota(shape):
    return lax.broadcasted_iota(jnp.int32, shape, 1)


def _mod_rows(modl_ref, modc_ref, base, count, tile_rows, n_ctx):
    row = pl.program_id(1) * tile_rows + _row_iota((tile_rows, 1))
    is_ctx = row < n_ctx
    return [jnp.where(is_ctx, modc_ref[0, base + i:base + i + 1, :], modl_ref[0, base + i:base + i + 1, :])
            for i in range(count)]


def _mod_specs(mod):
    ctx_row = mod.shape[0] - 1
    d = mod.shape[-1]
    return [pl.BlockSpec((1, N_MOD, d), lambda bi, ti: (bi, 0, 0)),
            pl.BlockSpec((1, N_MOD, d), lambda bi, ti: (ctx_row, 0, 0))]


def _mod_kernel(cond_ref, w_ref, b_ref, out_ref):
    cond = _silu(cond_ref[...])
    out_ref[0] = _dot(cond, w_ref[0]) + b_ref[0]


def _modulation(cond, w_ada, b_ada):
    depth, d, n = w_ada.shape
    tn = n // 8
    rows = cond.shape[0]
    return pl.pallas_call(
        _mod_kernel,
        out_shape=jax.ShapeDtypeStruct((depth, rows, n), F32),
        grid=(depth, n // tn),
        in_specs=[
            pl.BlockSpec((rows, d), lambda l, j: (0, 0)),
            pl.BlockSpec((1, d, tn), lambda l, j: (l, 0, j)),
            pl.BlockSpec((1, 1, tn), lambda l, j: (l, 0, j)),
        ],
        out_specs=pl.BlockSpec((1, rows, tn), lambda l, j: (l, 0, j)),
        compiler_params=_cparams(("arbitrary", "arbitrary")),
        name="adaln_mod",
    )(cond, w_ada, b_ada.reshape(depth, 1, n))


def _ffn_kernel(x_ref, modl_ref, modc_ref, w1g_ref, w1u_ref, w2_ref, g_ref, b_ref, out_ref, h_ref, *,
                mod_base, alpha, n_chunks, n_ctx):
    x = x_ref[0]
    shift, scale, gate = _mod_rows(modl_ref, modc_ref, mod_base, 3, x.shape[0], n_ctx)
    u = (_normalise(x) * (1.0 + scale) + shift).astype(BF16)
    for c in range(n_chunks):
        hg = _dot(u, w1g_ref[c])
        hu = _dot(u, w1u_ref[c])
        h_ref[:, c * FFN_CHUNK:(c + 1) * FFN_CHUNK] = (_silu(hg) * hu).astype(BF16)
    h = _dot(h_ref[...], w2_ref[...])
    y = alpha * x + FFN_RES * gate * h
    out_ref[0] = _normalise(y) * g_ref[...] + b_ref[...]


def _ffn_sublayer(x, mod, w1, w2, g, b, *, mod_base, alpha, n_ctx):
    bsz, nt, d = x.shape
    d_ff = w2.shape[0]
    n_chunks = d_ff // FFN_CHUNK
    tm = TM_FFN if nt % TM_FFN == 0 else 256
    w1b = w1.astype(BF16)
    w1g = w1b[:, :d_ff].reshape(d, n_chunks, FFN_CHUNK).transpose(1, 0, 2)
    w1u = w1b[:, d_ff:].reshape(d, n_chunks, FFN_CHUNK).transpose(1, 0, 2)
    w2b = w2.astype(BF16)
    kern = functools.partial(_ffn_kernel, mod_base=mod_base, alpha=alpha, n_chunks=n_chunks, n_ctx=n_ctx)
    return pl.pallas_call(
        kern,
        out_shape=jax.ShapeDtypeStruct((bsz, nt, d), F32),
        grid=(bsz, nt // tm),
        in_specs=[
            pl.BlockSpec((1, tm, d), lambda bi, ti: (bi, ti, 0)),
            *_mod_specs(mod),
            _const_spec((n_chunks, d, FFN_CHUNK)),
            _const_spec((n_chunks, d, FFN_CHUNK)),
            _const_spec((d_ff, d)),
            _const_spec((1, d)),
            _const_spec((1, d)),
        ],
        out_specs=pl.BlockSpec((1, tm, d), lambda bi, ti: (bi, ti, 0)),
        scratch_shapes=[pltpu.VMEM((tm, d_ff), BF16)],
        compiler_params=_cparams(("arbitrary", "arbitrary")),
        name="ffn_sublayer",
    )(x, mod, mod, w1g, w1u, w2b, g.reshape(1, d), b.reshape(1, d))


def _rope_tables(t_len, n_ctx):
    n = HEAD_DIM // 4
    inv = ROPE_BASE ** (-np.arange(n, dtype=np.float64) / n)
    pos = np.arange(t_len)
    row, col = pos // GRID_W, pos % GRID_W
    lane = np.arange(HEAD_DIM)
    freq = inv[lane % n]
    p = np.where((lane // (2 * n))[None, :] == 0, row[:, None], col[:, None]).astype(np.float64)
    ang = p * freq[None, :]
    sign = np.where((lane % (2 * n)) < n, -1.0, 1.0)
    cos = np.concatenate([np.ones((n_ctx, HEAD_DIM)), np.cos(ang)], axis=0)
    sin = np.concatenate([np.zeros((n_ctx, HEAD_DIM)), np.sin(ang) * sign[None, :]], axis=0)
    cos2 = np.tile(cos, (1, LANES // HEAD_DIM)).astype(np.float32)
    sin2 = np.tile(sin, (1, LANES // HEAD_DIM)).astype(np.float32)
    return jnp.asarray(cos2), jnp.asarray(sin2)


def _rot_perm(width):
    n = HEAD_DIM // 4
    lane = np.arange(width)
    return np.where((lane % (2 * n)) < n, lane + n, lane - n)


def _inproj_kernel(x_ref, modl_ref, modc_ref, cos_ref, sin_ref, w_ref,
                   q_ref, k_ref, v_ref, zr_ref, zs_ref, zg_ref, *, n_ctx, widths):
    wq, wk, wr, ws, wg = widths
    x = x_ref[0]
    shift, scale = _mod_rows(modl_ref, modc_ref, 3, 2, x.shape[0], n_ctx)
    u = (_normalise(x) * (1.0 + scale) + shift).astype(BF16)
    cos = cos_ref[...]
    sin = sin_ref[...]
    o = 0
    qscale = HEAD_DIM ** -0.5
    for j in range(wq // LANES):
        q = _dot(u, w_ref[:, o + j * LANES:o + (j + 1) * LANES])
        qr = _dot(u, w_ref[:, o + wq + j * LANES:o + wq + (j + 1) * LANES])
        q_ref[0, :, j * LANES:(j + 1) * LANES] = ((q * cos + qr * sin) * qscale).astype(BF16)
    o += 2 * wq
    k = _dot(u, w_ref[:, o:o + wk])
    kr = _dot(u, w_ref[:, o + wk:o + 2 * wk])
    k_ref[0] = (k * cos + kr * sin).astype(BF16)
    o += 2 * wk
    v_ref[0] = _dot(u, w_ref[:, o:o + wk]).astype(BF16)
    o += wk
    zr_ref[0] = _dot(u, w_ref[:, o:o + wr])
    o += wr
    zs_ref[0] = _dot(u, w_ref[:, o:o + ws])
    o += ws
    zg_ref[0] = _dot(u, w_ref[:, o:o + wg])


def _rwkv_pad_cols(a, axis):
    rw = 3 * 512 + 2 * W_LORA
    head = lax.slice_in_dim(a, 0, rw + A_LORA, axis=axis)
    tail = lax.slice_in_dim(a, rw + A_LORA, rw + A_LORA + G_LORA, axis=axis)
    pad_shape = list(a.shape)
    pad_shape[axis] = LANES - A_LORA
    return jnp.concatenate([head, jnp.zeros(pad_shape, a.dtype), tail], axis=axis)


def _inproj(x, mod, w_in, n_ctx):
    bsz, nt, d = x.shape
    wq = ATT_KV_HEADS * ATT_GROUP * HEAD_DIM
    wk = ATT_KV_HEADS * HEAD_DIM
    att_in = wq + 2 * wk
    rwkv_in = 3 * 512 + 2 * W_LORA + A_LORA + G_LORA
    ws = 512
    wg = 3 * d
    w_q = w_in[:, :wq]
    w_k = w_in[:, wq:wq + wk]
    w_v = w_in[:, wq + wk:att_in]
    w_r = _rwkv_pad_cols(w_in[:, att_in:att_in + rwkv_in], 1)
    wr = w_r.shape[1]
    w_s = w_in[:, att_in + rwkv_in:att_in + rwkv_in + ws]
    w_g = w_in[:, att_in + rwkv_in + ws:]
    w_all = jnp.concatenate([w_q, w_q[:, _rot_perm(wq)], w_k, w_k[:, _rot_perm(wk)], w_v, w_r, w_s, w_g],
                            axis=1).astype(BF16)
    cos, sin = _rope_tables(nt - n_ctx, n_ctx)
    tm = TM_PROJ
    kern = functools.partial(_inproj_kernel, n_ctx=n_ctx, widths=(wq, wk, wr, ws, wg))
    tok = lambda w: pl.BlockSpec((1, tm, w), lambda bi, ti: (bi, ti, 0))
    return pl.pallas_call(
        kern,
        out_shape=[jax.ShapeDtypeStruct((bsz, nt, wq), BF16),
                   jax.ShapeDtypeStruct((bsz, nt, wk), BF16),
                   jax.ShapeDtypeStruct((bsz, nt, wk), BF16),
                   jax.ShapeDtypeStruct((bsz, nt, wr), F32),
                   jax.ShapeDtypeStruct((bsz, nt, ws), F32),
                   jax.ShapeDtypeStruct((bsz, nt, wg), F32)],
        grid=(bsz, nt // tm),
        in_specs=[tok(d), *_mod_specs(mod),
                  pl.BlockSpec((tm, LANES), lambda bi, ti: (ti, 0)),
                  pl.BlockSpec((tm, LANES), lambda bi, ti: (ti, 0)),
                  _const_spec(w_all.shape)],
        out_specs=[tok(wq), tok(wk), tok(wk), tok(wr), tok(ws), tok(wg)],
        compiler_params=_cparams(("arbitrary", "arbitrary")),
        name="mixer_inproj",
    )(x, mod, mod, cos, sin, w_all)


def _attn_kernel(sink_ref, q_ref, kp_ref, ko_ref, kn_ref, kc_ref, vp_ref, vo_ref, vn_ref, vc_ref, out_ref, *,
                 n_ctx_blocks, n_blocks):
    j = pl.program_id(1)
    blk = ATT_BLOCK
    rows = ATT_GROUP * blk
    q = q_ref[0]
    kc = kc_ref[0]
    vc = vc_ref[0]

    def head_rows(h):
        return jnp.concatenate(
            [q[:, (h * ATT_GROUP + g) * HEAD_DIM:(h * ATT_GROUP + g + 1) * HEAD_DIM] for g in range(ATT_GROUP)], axis=0)

    def sink_col(h):
        g_of_row = _row_iota((rows, 1)) // blk
        col = jnp.full((rows, 1), sink_ref[h * ATT_GROUP], F32)
        for g in range(1, ATT_GROUP):
            col = jnp.where(g_of_row == g, sink_ref[h * ATT_GROUP + g], col)
        return col

    def write(outs):
        pieces = [outs[h][g * blk:(g + 1) * blk, :] for h in range(ATT_KV_HEADS) for g in range(ATT_GROUP)]
        out_ref[0] = jnp.concatenate(pieces, axis=1)

    @pl.when(j < n_ctx_blocks)
    def _():
        outs = []
        for h in range(ATT_KV_HEADS):
            sl = slice(h * HEAD_DIM, (h + 1) * HEAD_DIM)
            qs = head_rows(h)
            s = _dot_nt(qs, kc[:, sl])
            sk = sink_col(h)
            m = jnp.maximum(jnp.max(s, axis=-1, keepdims=True), sk)
            p = jnp.exp(s - m)
            den = jnp.sum(p, axis=-1, keepdims=True) + jnp.exp(sk - m)
            outs.append(_dot(p.astype(BF16), vc[:, sl]) / den)
        write(outs)

    @pl.when(j >= n_ctx_blocks)
    def _():
        lo = jnp.where(j > n_ctx_blocks, 0, blk)
        hi = jnp.where(j < n_blocks - 1, 3 * blk, 2 * blk)
        q_off = _row_iota((rows, 3 * blk)) % blk
        k_off = _col_iota((rows, 3 * blk))
        rel = k_off - blk - q_off
        ok = (jnp.abs(rel) <= WINDOW) & (k_off >= lo) & (k_off < hi)
        outs = []
        for h in range(ATT_KV_HEADS):
            sl = slice(h * HEAD_DIM, (h + 1) * HEAD_DIM)
            qs = head_rows(h)
            k_loc = jnp.concatenate([kp_ref[0][:, sl], ko_ref[0][:, sl], kn_ref[0][:, sl]], axis=0)
            v_loc = jnp.concatenate([vp_ref[0][:, sl], vo_ref[0][:, sl], vn_ref[0][:, sl]], axis=0)
            s_loc = jnp.where(ok, _dot_nt(qs, k_loc), NEG_INF)
            s_ctx = _dot_nt(qs, kc[:, sl])
            sk = sink_col(h)
            m = jnp.maximum(jnp.maximum(jnp.max(s_loc, axis=-1, keepdims=True),
                                        jnp.max(s_ctx, axis=-1, keepdims=True)), sk)
            p_loc = jnp.exp(s_loc - m)
            p_ctx = jnp.exp(s_ctx - m)
            den = (jnp.sum(p_loc, axis=-1, keepdims=True) + jnp.sum(p_ctx, axis=-1, keepdims=True)
                   + jnp.exp(sk - m))
            o = _dot(p_loc.astype(BF16), v_loc) + _dot(p_ctx.astype(BF16), vc[:, sl])
            outs.append(o / den)
        write(outs)


def _attention(q, k, v, sink, n_ctx):
    bsz, nt, wq = q.shape
    wk = k.shape[-1]
    blk = ATT_BLOCK
    nb = nt // blk
    ncb = n_ctx // blk
    kern = functools.partial(_attn_kernel, n_ctx_blocks=ncb, n_blocks=nb)
    prev_map = lambda bi, j: (bi, jnp.clip(j - 1, ncb, nb - 1), 0)
    own_map = lambda bi, j: (bi, j, 0)
    next_map = lambda bi, j: (bi, jnp.clip(j + 1, ncb, nb - 1), 0)
    ctx_map = lambda bi, j: (bi, 0, 0)
    kv = lambda m: pl.BlockSpec((1, blk, wk), m)
    ctx_spec = pl.BlockSpec((1, n_ctx, wk), ctx_map)
    return pl.pallas_call(
        kern,
        out_shape=jax.ShapeDtypeStruct((bsz, nt, wq), F32),
        grid=(bsz, nb),
        in_specs=[pl.BlockSpec(memory_space=pltpu.SMEM),
                  pl.BlockSpec((1, blk, wq), own_map),
                  kv(prev_map), kv(own_map), kv(next_map), ctx_spec,
                  kv(prev_map), kv(own_map), kv(next_map), ctx_spec],
        out_specs=pl.BlockSpec((1, blk, wq), own_map),
        compiler_params=_cparams(("arbitrary", "arbitrary")),
        name="window_attention",
    )(sink, q, k, k, k, k, v, v, v, v)


RW_W = 512
RW_PAIRS = RW_W // LANES
RW_COLS = 3 * RW_W + 3 * LANES


def _head_sum(x, ones_bd):
    return _split_dot(x, ones_bd)


def _pair_stack(x):
    is_a = _col_iota(x.shape) < HEAD_DIM
    return jnp.concatenate([jnp.where(is_a, x, 0.0), jnp.where(is_a, 0.0, x)], axis=0)


def _rwkv_unit(qh, rh, kt, bt, v, e_row, reverse):
    n = 2 * RW_CHUNK
    qs, rs, ks, bs, vs = (_pair_stack(t) for t in (qh, rh, kt, bt, v))
    qs_b, rs_b, ks_b, bs_b, vs_b = (t.astype(BF16) for t in (qs, rs, ks, bs, vs))
    r_i = _row_iota((n, n))
    c_i = _col_iota((n, n))
    if reverse:
        strict, incl = c_i > r_i, c_i >= r_i
    else:
        strict, incl = c_i < r_i, c_i <= r_i
    nmat = jnp.where(strict, _dot_nt(qs_b, bs_b), 0.0)
    a_k = jnp.where(strict, _dot_nt(qs_b, ks_b), 0.0).astype(BF16)
    a_ki = jnp.where(incl, _dot_nt(rs_b, ks_b), 0.0).astype(BF16)
    a_bi = jnp.where(incl, _dot_nt(rs_b, bs_b), 0.0).astype(BF16)

    def level_mask(log_s):
        same = (r_i >> (log_s + 1)) == (c_i >> (log_s + 1))
        r_hi = ((r_i >> log_s) & 1) == 1
        c_hi = ((c_i >> log_s) & 1) == 1
        if reverse:
            return same & jnp.logical_not(r_hi) & c_hi
        return same & r_hi & jnp.logical_not(c_hi)

    x = jnp.where(r_i == c_i, 1.0, 0.0) - jnp.where(level_mask(0), nmat, 0.0)
    for log_s in range(1, int(math.log2(RW_CHUNK))):
        low = jnp.where(level_mask(log_s), nmat, 0.0).astype(BF16)
        xb = x.astype(BF16)
        x = x - _dot(xb, _dot(low, xb).astype(BF16))
    p_b = x.astype(BF16)

    akv = _dot(a_k, vs_b)
    pu = _dot(p_b, jnp.concatenate([akv, qs], axis=1).astype(BF16))
    u_loc, q_eff = pu[:, :LANES], pu[:, LANES:]
    abu = _dot(a_bi, pu.astype(BF16))
    o_s = _dot(a_ki, vs_b) - abu[:, :LANES]
    r_s = rs - abu[:, LANES:]
    o_loc = o_s[:RW_CHUNK] + o_s[RW_CHUNK:]
    r_eff = (r_s[:RW_CHUNK] + r_s[RW_CHUNK:]) * e_row
    ksf = (ks * e_row).astype(BF16)
    bsf = (bs * e_row).astype(BF16)
    g_mat = _dot_tn(ksf, vs_b) - _dot_tn(bsf, u_loc.astype(BF16))
    diag = jnp.where(r_i == c_i, jnp.broadcast_to(e_row * e_row, (n, n)), 0.0)
    m_mat = diag - _dot_tn(bsf, (q_eff * e_row).astype(BF16))
    return o_loc, r_eff, m_mat, g_mat


def _rwkv_local_kernel(z_ref, zp_ref, zn_ref, mu_ref, w0_ref, w2_ref, a0_ref, a2_ref, g2_ref, kk_ref, ka_ref,
                       rk_ref, ones_ref, tri_ref, half_ref,
                       oloc_ref, rf_ref, rb_ref, bonus_ref, g_ref, m_ref, gm_ref,
                       kt_s, bt_s, qh_s, rh_s, e_s, v_s, *, n_ctx_tiles, n_tiles):
    ti = pl.program_id(1)
    tr = z_ref.shape[1]
    z = z_ref[0]
    first = (ti == 0) | (ti == n_ctx_tiles)
    last = (ti == n_ctx_tiles - 1) | (ti == n_tiles - 1)
    prev_row = jnp.where(first, 0.0, zp_ref[0, 7:8, :])
    next_row = jnp.where(last, 0.0, zn_ref[0, 0:1, :])
    rowi = _row_iota((tr, 1))
    zp = jnp.where(rowi == 0, prev_row, pltpu.roll(z, 1, axis=0))
    zn = jnp.where(rowi == tr - 1, next_row, pltpu.roll(z, tr - 1, axis=0))
    zm = z + mu_ref[...] * (0.5 * (zp + zn) - z)

    w = RW_W
    r, k, v = zm[:, 0:w], zm[:, w:2 * w], zm[:, 2 * w:3 * w]
    wl = zm[:, 3 * w:3 * w + LANES]
    al = zm[:, 3 * w + LANES:3 * w + 2 * LANES]
    gl = zm[:, 3 * w + 2 * LANES:3 * w + 3 * LANES]
    lw = -DECAY_SCALE * _sigmoid(w0_ref[...] + _dot(jnp.tanh(wl), w2_ref[...]))
    a = _sigmoid(a0_ref[...] + _dot(al, a2_ref[...]))
    g_ref[0] = _dot(_sigmoid(gl), g2_ref[...])
    ones_bd = ones_ref[...]
    kkr = k * kk_ref[...]
    kk = kkr * lax.rsqrt(_head_sum(kkr * kkr, ones_bd) + 1e-12)
    k2 = k * (1.0 + (a - 1.0) * ka_ref[...])
    bm = kk * a
    bonus_ref[0] = _head_sum(r * k2 * rk_ref[...], ones_bd) * v
    v_s[...] = v
    for d in range(2):
        lwd = lw[:, d * w:(d + 1) * w]
        mid = _dot(half_ref[...], lwd)
        cb = _dot(tri_ref[d], lwd) - mid
        en = jnp.exp(-cb)
        kt_s[d] = k2 * en
        bt_s[d] = bm * en
        qh_s[d] = kk * jnp.exp(cb - lwd)
        rh_s[d] = r * jnp.exp(cb)
        e_s[d] = jnp.exp(mid)

    def chunk_body(c, carry):
        r0 = pl.multiple_of(c * RW_CHUNK, RW_CHUNK)
        rows = pl.ds(r0, RW_CHUNK)
        for p in range(RW_PAIRS):
            lanes = slice(p * LANES, (p + 1) * LANES)
            vv = v_s[rows, lanes]
            o_sum = None
            for d, r_out in ((0, rf_ref), (1, rb_ref)):
                e_row = e_s[d, pl.ds(r0, 1), lanes]
                o_loc, r_eff, m_mat, g_mat = _rwkv_unit(qh_s[d, rows, lanes], rh_s[d, rows, lanes],
                                                        kt_s[d, rows, lanes], bt_s[d, rows, lanes],
                                                        vv, e_row, reverse=(d == 1))
                r_out[0, rows, lanes] = r_eff.astype(BF16)
                m_ref[0, d, c, p] = m_mat
                gm_ref[0, d, c, p] = g_mat
                o_sum = o_loc if o_sum is None else o_sum + o_loc
            oloc_ref[0, rows, lanes] = o_sum
        return carry

    lax.fori_loop(0, tr // RW_CHUNK, chunk_body, 0)


def _rwkv_state_kernel(m_ref, g_ref, h0_ref, h_s):
    @pl.when(pl.program_id(1) == 0)
    def _():
        h_s[...] = jnp.zeros(h_s.shape, F32)

    for b in range(h_s.shape[0]):
        for p in range(h_s.shape[1]):
            h = h_s[b, p]
            hb = h.astype(BF16)
            h0_ref[b, 0, 0, p] = hb
            h_s[b, p] = _dot(m_ref[b, 0, 0, p].astype(BF16), hb) + g_ref[b, 0, 0, p]


def _rwkv_out_kernel(oloc_ref, rf_ref, rb_ref, h0_ref, bonus_ref, g_ref, gng_ref, gnb_ref, ones_ref, out_ref, o_s):
    tr = oloc_ref.shape[1]
    for c in range(tr // RW_CHUNK):
        rows = slice(c * RW_CHUNK, (c + 1) * RW_CHUNK)
        for p in range(RW_PAIRS):
            lanes = slice(p * LANES, (p + 1) * LANES)
            o_s[rows, lanes] = (oloc_ref[0, rows, lanes]
                                + _dot(rf_ref[0, rows, lanes], h0_ref[0, 0, c, p])
                                + _dot(rb_ref[0, rows, lanes], h0_ref[0, 1, c, p]))
    o = o_s[...]
    ones_bd = ones_ref[...]
    mean = _head_sum(o, ones_bd) * (1.0 / HEAD_DIM)
    oc = o - mean
    var = _head_sum(oc * oc, ones_bd) * (1.0 / HEAD_DIM)
    on = oc * lax.rsqrt(var + GN_EPS) * gng_ref[...] + gnb_ref[...]
    out_ref[0] = (on + bonus_ref[0]) * g_ref[0]


def _rwkv_branch(zr, p, n_ctx):
    bsz, nt, wz = zr.shape
    w = RW_W
    tr = RW_TILE
    n_tiles = nt // tr
    nct = n_ctx // tr
    nch = nt // RW_CHUNK
    ncc = n_ctx // RW_CHUNK
    cpt = tr // RW_CHUNK
    row = lambda a: a.reshape(1, -1).astype(F32)
    mu = row(_rwkv_pad_cols(p['rwkv_mu'], 0))
    w0 = row(p['rwkv_w0'])
    w2 = jnp.zeros((LANES, 2 * w), F32)
    w2 = w2.at[:W_LORA, :w].set(p['rwkv_w2'][0]).at[W_LORA:, w:].set(p['rwkv_w2'][1])
    a2 = jnp.zeros((LANES, w), F32).at[:A_LORA].set(p['rwkv_a2'])
    hid = np.arange(w) // HEAD_DIM
    ones_bd = jnp.asarray(hid[:, None] == hid[None, :], BF16)
    ti_ = np.arange(tr)
    same = (ti_[:, None] // RW_CHUNK) == (ti_[None, :] // RW_CHUNK)
    tri = jnp.asarray(np.stack([same & (ti_[None, :] <= ti_[:, None]), same & (ti_[None, :] >= ti_[:, None])]), F32)
    half = jnp.asarray(same, F32) * 0.5

    halo = 8
    tok = lambda width: pl.BlockSpec((1, tr, width), lambda bi, ti: (bi, ti, 0))
    mg_spec = pl.BlockSpec((1, 2, cpt, RW_PAIRS, LANES, LANES), lambda bi, ti: (bi, 0, ti, 0, 0, 0))
    kern = functools.partial(_rwkv_local_kernel, n_ctx_tiles=nct, n_tiles=n_tiles)
    mg_shape = jax.ShapeDtypeStruct((bsz, 2, nch, RW_PAIRS, LANES, LANES), F32)
    oloc, rf, rb, bonus, g, m_all, g_all = pl.pallas_call(
        kern,
        out_shape=[jax.ShapeDtypeStruct((bsz, nt, w), F32),
                   jax.ShapeDtypeStruct((bsz, nt, w), BF16),
                   jax.ShapeDtypeStruct((bsz, nt, w), BF16),
                   jax.ShapeDtypeStruct((bsz, nt, w), F32),
                   jax.ShapeDtypeStruct((bsz, nt, w), F32),
                   mg_shape, mg_shape],
        grid=(bsz, n_tiles),
        in_specs=[tok(wz),
                  pl.BlockSpec((1, halo, wz), lambda bi, ti: (bi, jnp.maximum(ti * (tr // halo) - 1, 0), 0)),
                  pl.BlockSpec((1, halo, wz), lambda bi, ti: (bi, jnp.minimum((ti + 1) * (tr // halo), nt // halo - 1), 0)),
                  _const_spec(mu.shape), _const_spec(w0.shape), _const_spec(w2.shape),
                  _const_spec((1, w)), _const_spec(a2.shape), _const_spec((G_LORA, w)),
                  _const_spec((1, w)), _const_spec((1, w)), _const_spec((1, w)),
                  _const_spec(ones_bd.shape), _const_spec(tri.shape), _const_spec(half.shape)],
        out_specs=[tok(w), tok(w), tok(w), tok(w), tok(w), mg_spec, mg_spec],
        scratch_shapes=[pltpu.VMEM((2, tr, w), F32)] * 5 + [pltpu.VMEM((tr, w), F32)],
        compiler_params=_cparams(("arbitrary", "arbitrary")),
        name="rwkv_local",
    )(zr, zr, zr, mu, w0, w2, row(p['rwkv_a0']), a2, p['rwkv_g2'].astype(F32),
      row(p['rwkv_k_k']), row(p['rwkv_k_a']), row(p['rwkv_r_k']), ones_bd, tri, half)

    def chunk_of(d, i):
        bwd = jnp.where(i < ncc, ncc - 1 - i, nch - 1 - (i - ncc))
        return jnp.where(d == 0, i, bwd)

    st_spec = pl.BlockSpec((bsz, 1, 1, RW_PAIRS, LANES, LANES), lambda d, i: (0, d, chunk_of(d, i), 0, 0, 0))
    h0 = pl.pallas_call(
        _rwkv_state_kernel,
        out_shape=jax.ShapeDtypeStruct((bsz, 2, nch, RW_PAIRS, LANES, LANES), BF16),
        grid=(2, nch),
        in_specs=[st_spec, st_spec],
        out_specs=st_spec,
        scratch_shapes=[pltpu.VMEM((bsz, RW_PAIRS, LANES, LANES), F32)],
        compiler_params=_cparams(("arbitrary", "arbitrary")),
        name="rwkv_state",
    )(m_all, g_all)

    return pl.pallas_call(
        _rwkv_out_kernel,
        out_shape=jax.ShapeDtypeStruct((bsz, nt, w), F32),
        grid=(bsz, n_tiles),
        in_specs=[tok(w), tok(w), tok(w), mg_spec, tok(w), tok(w),
                  _const_spec((1, w)), _const_spec((1, w)), _const_spec(ones_bd.shape)],
        out_specs=tok(w),
        scratch_shapes=[pltpu.VMEM((tr, w), F32)],
        compiler_params=_cparams(("arbitrary", "arbitrary")),
        name="rwkv_out",
    )(oloc, rf, rb, h0, bonus, g, row(p['rwkv_gn_g']), row(p['rwkv_gn_b']), ones_bd)


S5_W = 512
S5_BLOCKS = S5_W // LANES
S5_GPB = LANES // S5_GROUP_CH
S5_HALF = S5_GPB * S5_STATE


def _s5_param_kernel(are_ref, aim_ref, lstep_ref, bre_ref, bim_ref,
                     bbr_ref, bbi_ref, tnr_ref, tni_ref, tpr_ref, tpi_ref, lcr_ref, lci_ref):
    rows = tnr_ref.shape[1]
    rowf = _row_iota((rows, 1)).astype(F32)
    for d in range(2):
        ar = are_ref[d:d + 1, :]
        ai = aim_ref[d:d + 1, :]
        dt = jnp.exp(lstep_ref[d:d + 1, :])
        mag = jnp.exp(ar * dt)
        lr = mag * jnp.cos(ai * dt)
        li = mag * jnp.sin(ai * dt)
        den = ar * ar + ai * ai
        cr = ((lr - 1.0) * ar + li * ai) / den
        ci = (li * ar - (lr - 1.0) * ai) / den
        bbr_ref[d] = cr * bre_ref[...] - ci * bim_ref[...]
        bbi_ref[d] = cr * bim_ref[...] + ci * bre_ref[...]
        if d == 0:
            e_neg, e_pos = S5_MID - rowf, rowf - S5_MID
        else:
            e_neg, e_pos = rowf - (rows - 1 - S5_MID), (rows - 1 - S5_MID) - rowf
        for e, out_r, out_i in ((e_neg, tnr_ref, tni_ref), (e_pos, tpr_ref, tpi_ref)):
            m = jnp.exp(e * (ar * dt))
            ph = e * (ai * dt)
            out_r[d] = m * jnp.cos(ph)
            out_i[d] = m * jnp.sin(ph)
        ec = float(S5_MID + 1)
        mc = jnp.exp(ec * (ar * dt))
        lcr_ref[d:d + 1, :] = mc * jnp.cos(ec * (ai * dt))
        lci_ref[d:d + 1, :] = mc * jnp.sin(ec * (ai * dt))


def _s5_state_lanes(t):
    re, im = t
    lead = re.shape[:-1]
    re = re.reshape(lead + (S5_BLOCKS, S5_HALF))
    im = im.reshape(lead + (S5_BLOCKS, S5_HALF))
    return jnp.concatenate([re, im], axis=-1)


def _s5_params(p):
    a_re, a_im, lstep = p['s5_a_re'], p['s5_a_im'], p['s5_log_step']
    n_g, n_p = a_re.shape[1], a_re.shape[2]
    n_c = p['s5_b_re'].shape[-1]
    gp = n_g * n_p
    flat = lambda a: a.reshape(2, gp).astype(F32)
    lstep_rep = jnp.repeat(lstep.astype(F32), n_p, axis=1)
    b_t = lambda a: a.astype(F32).reshape(gp, n_c).T
    vec = jax.ShapeDtypeStruct((2, gp), F32)
    tab = jax.ShapeDtypeStruct((2, S5_CHUNK, gp), F32)
    bsh = jax.ShapeDtypeStruct((2, n_c, gp), F32)
    bbr, bbi, tnr, tni, tpr, tpi, lcr, lci = pl.pallas_call(
        _s5_param_kernel,
        out_shape=[bsh, bsh, tab, tab, tab, tab, vec, vec],
        name="s5_params",
    )(flat(a_re), flat(a_im), lstep_rep, b_t(p['s5_b_re']), b_t(p['s5_b_im']))

    eye = jnp.eye(S5_GPB, dtype=F32)

    def drive(bb):
        t = bb.reshape(2, n_c, S5_BLOCKS, S5_GPB, n_p).transpose(0, 2, 3, 1, 4)
        t = t[:, :, :, :, None, :] * eye[None, None, :, None, :, None]
        return t.reshape(2, S5_BLOCKS, LANES, S5_HALF)

    wb = jnp.concatenate([drive(bbr), drive(bbi)], axis=-1).astype(BF16)

    def readout(cm):
        t = cm.astype(F32).reshape(S5_BLOCKS, S5_GPB, n_c, n_p).transpose(0, 1, 3, 2)
        t = t[:, :, :, None, :] * eye[None, :, None, :, None]
        return t.reshape(S5_BLOCKS, S5_HALF, LANES)

    wc = jnp.concatenate([readout(p['s5_c_re']), -readout(p['s5_c_im'])], axis=1).astype(BF16)
    tn = _s5_state_lanes((tnr, tni)).transpose(0, 2, 1, 3)
    tp = _s5_state_lanes((tpr, tpi)).transpose(0, 2, 1, 3)
    lc = _s5_state_lanes((lcr, lci))[:, :, None, :]
    return wb, wc, tn, tp, lc


def _gelu_tanh(y):
    return 0.5 * y * (1.0 + jnp.tanh(0.7978845608028654 * (y + 0.044715 * (y * y * y))))


def _s5_scan_kernel(*refs, reverse, finish):
    if finish:
        (u_ref, wb_ref, tn_ref, tp_ref, lc_ref, tri_ref, wc_ref, yf_ref, d_ref, gw_ref, gb_ref,
         out_ref, carry_s, y_s) = refs
    else:
        u_ref, wb_ref, tn_ref, tp_ref, lc_ref, tri_ref, wc_ref, out_ref, carry_s, y_s = refs

    @pl.when(pl.program_id(1) == 0)
    def _():
        carry_s[...] = jnp.zeros(carry_s.shape, F32)

    ts = u_ref.shape[1]
    n_sub = ts // S5_CHUNK
    order = range(n_sub - 1, -1, -1) if reverse else range(n_sub)
    last = 0 if reverse else S5_CHUNK - 1
    hs = S5_HALF
    tri = tri_ref[...]
    for blk in range(S5_BLOCKS):
        lanes = slice(blk * LANES, (blk + 1) * LANES)
        carry = carry_s[blk]
        tnr, tni = tn_ref[0, blk, :, :hs], tn_ref[0, blk, :, hs:]
        tpr, tpi = tp_ref[0, blk, :, :hs], tp_ref[0, blk, :, hs:]
        lcr, lci = lc_ref[0, blk, :, :hs], lc_ref[0, blk, :, hs:]
        for sc in order:
            rows = slice(sc * S5_CHUNK, (sc + 1) * S5_CHUNK)
            bu = _dot(u_ref[0, rows, lanes].astype(BF16), wb_ref[0, blk])
            bur, bui = bu[:, :hs], bu[:, hs:]
            bs = jnp.concatenate([bur * tnr - bui * tni, bur * tni + bui * tnr], axis=1).astype(BF16)
            cum = _dot(tri, bs)
            car, cai = carry[:, :hs], carry[:, hs:]
            cumr = cum[:, :hs] + (lcr * car - lci * cai)
            cumi = cum[:, hs:] + (lcr * cai + lci * car)
            x = jnp.concatenate([tpr * cumr - tpi * cumi, tpr * cumi + tpi * cumr], axis=1)
            carry = x[last:last + 1, :]
            y_s[rows, lanes] = _dot(x.astype(BF16), wc_ref[blk])
        carry_s[blk] = carry
    if finish:
        u = u_ref[0]
        y = _gelu_tanh(y_s[...] + yf_ref[0] + d_ref[...] * u)
        gate = _sigmoid(_dot(y.astype(BF16), gw_ref[...]) + gb_ref[...])
        out_ref[0] = y * gate
    else:
        out_ref[0] = y_s[...]


def _s5_branch(zs, p, n_ctx):
    bsz, nt, w = zs.shape
    ts = S5_TILE
    n_tiles = nt // ts
    nct = n_ctx // ts
    wb, wc, tn, tp, lc = _s5_params(p)
    ci = np.arange(S5_CHUNK)
    tri_f = jnp.asarray(ci[None, :] <= ci[:, None], BF16)
    tri_b = jnp.asarray(ci[None, :] >= ci[:, None], BF16)
    row = lambda a: a.reshape(1, -1).astype(F32)

    def run(d, tri, extra_in, extra_specs):
        if d == 0:
            tile_of = lambda i: i
        else:
            tile_of = lambda i: jnp.where(i < nct, nct - 1 - i, n_tiles - 1 - (i - nct))
        tok = pl.BlockSpec((1, ts, w), lambda bi, i: (bi, tile_of(i), 0))
        dsel = lambda shape: pl.BlockSpec((1,) + shape, lambda bi, i: (d,) + (0,) * len(shape))
        kern = functools.partial(_s5_scan_kernel, reverse=(d == 1), finish=(d == 1))
        specs = [tok, dsel(wb.shape[1:]), dsel(tn.shape[1:]), dsel(tp.shape[1:]), dsel(lc.shape[1:]),
                 _const_spec(tri.shape), _const_spec(wc.shape)]
        if d == 1:
            specs = specs + [tok] + extra_specs
        return pl.pallas_call(
            kern,
            out_shape=jax.ShapeDtypeStruct((bsz, nt, w), F32),
            grid=(bsz, n_tiles),
            in_specs=specs,
            out_specs=tok,
            scratch_shapes=[pltpu.VMEM((S5_BLOCKS, 1, 2 * S5_HALF), F32), pltpu.VMEM((ts, w), F32)],
            compiler_params=_cparams(("arbitrary", "arbitrary")),
            name="s5_scan_bwd" if d else "s5_scan_fwd",
        )(zs, wb, tn, tp, lc, tri, wc, *extra_in)

    y_f = run(0, tri_f, [], [])
    return run(1, tri_b, [y_f, row(p['s5_d']), p['s5_glu_w'].astype(BF16), row(p['s5_glu_b'])],
               [_const_spec((1, w)), _const_spec((w, w)), _const_spec((1, w))])


def _merge_kernel(x_ref, modl_ref, modc_ref, zg_ref, ya_ref, yr_ref, ys_ref, bp_ref, wo_ref, g_ref, b_ref,
                  out_ref, *, alpha, n_ctx):
    x = x_ref[0]
    d = x.shape[1]
    (gate,) = _mod_rows(modl_ref, modc_ref, 5, 1, x.shape[0], n_ctx)
    m = None
    for i, y_ref in enumerate((ya_ref, yr_ref, ys_ref)):
        t = _sigmoid(zg_ref[0, :, i * d:(i + 1) * d]) * _dot(y_ref[0].astype(BF16), bp_ref[i])
        m = t if m is None else m + t
    y = alpha * x + gate * _dot(m.astype(BF16), wo_ref[...])
    out_ref[0] = _normalise(y) * g_ref[...] + b_ref[...]


def _merge(x, mod, zg, ya, yr, ys, branch_proj, w_out, g, b, *, alpha, n_ctx):
    bsz, nt, d = x.shape
    wy = ya.shape[-1]
    tm = TM_MERGE if nt % TM_MERGE == 0 else 256
    tok = lambda width: pl.BlockSpec((1, tm, width), lambda bi, ti: (bi, ti, 0))
    kern = functools.partial(_merge_kernel, alpha=alpha, n_ctx=n_ctx)
    return pl.pallas_call(
        kern,
        out_shape=jax.ShapeDtypeStruct((bsz, nt, d), F32),
        grid=(bsz, nt // tm),
        in_specs=[tok(d), *_mod_specs(mod), tok(3 * d), tok(wy), tok(wy), tok(wy),
                  _const_spec(branch_proj.shape), _const_spec(w_out.shape), _const_spec((1, d)), _const_spec((1, d))],
        out_specs=tok(d),
        compiler_params=_cparams(("arbitrary", "arbitrary")),
        name="gated_merge",
    )(x, mod, mod, zg, ya, yr, ys, branch_proj.astype(BF16), w_out.astype(BF16), g.reshape(1, d), b.reshape(1, d))


_LAYER_KEYS = ('ln_g', 'ln_b', 'ffn_w_in', 'ffn_w_out', 'w_in', 'attn_sink', 'rwkv_mu', 'rwkv_w0', 'rwkv_w2',
               'rwkv_a0', 'rwkv_a2', 'rwkv_g2', 'rwkv_k_k', 'rwkv_k_a', 'rwkv_r_k', 'rwkv_gn_g', 'rwkv_gn_b',
               's5_a_re', 's5_a_im', 's5_log_step', 's5_b_re', 's5_b_im', 's5_c_re', 's5_c_im', 's5_d',
               's5_glu_w', 's5_glu_b', 'branch_proj', 'w_out')


def _mixer(x, mod, p, n_ctx, alpha):
    q, k, v, zr, zs, zg = _inproj(x, mod, p['w_in'], n_ctx)
    ya = _attention(q, k, v, p['attn_sink'].astype(F32), n_ctx)
    yr = _rwkv_branch(zr, p, n_ctx)
    ys = _s5_branch(zs, p, n_ctx)
    out = _merge(x, mod, zg, ya, yr, ys, p['branch_proj'], p['w_out'], p['ln_g'][1], p['ln_b'][1],
                 alpha=alpha, n_ctx=n_ctx)
    return dict(ya=ya, yr=yr, ys=ys, out=out)


def _mixer_debug(x, mod, p, n_ctx):
    return _mixer(x, mod, p, n_ctx, (2.0 * 2) ** 0.25)


def kernel(x, c, ctx, c_ctx, w_ada, b_ada, ln_g, ln_b, ffn_w_in, ffn_w_out, w_in, attn_sink, rwkv_mu, rwkv_w0,
           rwkv_w2, rwkv_a0, rwkv_a2, rwkv_g2, rwkv_k_k, rwkv_k_a, rwkv_r_k, rwkv_gn_g, rwkv_gn_b, s5_a_re,
           s5_a_im, s5_log_step, s5_b_re, s5_b_im, s5_c_re, s5_c_im, s5_d, s5_glu_w, s5_glu_b, branch_proj,
           w_out):
    params = dict(ln_g=ln_g, ln_b=ln_b, ffn_w_in=ffn_w_in, ffn_w_out=ffn_w_out, w_in=w_in, attn_sink=attn_sink,
                  rwkv_mu=rwkv_mu, rwkv_w0=rwkv_w0, rwkv_w2=rwkv_w2, rwkv_a0=rwkv_a0, rwkv_a2=rwkv_a2,
                  rwkv_g2=rwkv_g2, rwkv_k_k=rwkv_k_k, rwkv_k_a=rwkv_k_a, rwkv_r_k=rwkv_r_k, rwkv_gn_g=rwkv_gn_g,
                  rwkv_gn_b=rwkv_gn_b, s5_a_re=s5_a_re, s5_a_im=s5_a_im, s5_log_step=s5_log_step,
                  s5_b_re=s5_b_re, s5_b_im=s5_b_im, s5_c_re=s5_c_re, s5_c_im=s5_c_im, s5_d=s5_d,
                  s5_glu_w=s5_glu_w, s5_glu_b=s5_glu_b, branch_proj=branch_proj, w_out=w_out)
    bsz, t_len, d = x.shape
    n_ctx = ctx.shape[1]
    depth = w_ada.shape[0]
    assert n_ctx % RW_TILE == 0 and t_len % RW_TILE == 0 and t_len % GRID_W == 0
    alpha = (2.0 * depth) ** 0.25
    cond = jnp.concatenate([c, c_ctx[None, :]], axis=0).astype(F32)
    mod = _modulation(cond, w_ada, b_ada).reshape(depth, bsz + 1, N_MOD, d)
    xs = jnp.concatenate([ctx, x], axis=1)
    for l in range(depth):
        p = {key: params[key][l] for key in _LAYER_KEYS}
        xs = _ffn_sublayer(xs, mod[l], p['ffn_w_in'][0], p['ffn_w_out'][0], p['ln_g'][0], p['ln_b'][0],
                           mod_base=0, alpha=alpha, n_ctx=n_ctx)
        xs = _mixer(xs, mod[l], p, n_ctx, alpha)['out']
        xs = _ffn_sublayer(xs, mod[l], p['ffn_w_in'][1], p['ffn_w_out'][1], p['ln_g'][2], p['ln_b'][2],
                           mod_base=6, alpha=alpha, n_ctx=n_ctx)
    return xs[:, n_ctx:, :]
```
